```python
import jax, jax.numpy as jnp
from jax import lax
import numpy as np

D_MODEL = 1024
BATCH = 16
SEQ = 4096
DEPTH = 1
DEC_BATCH = 8
DEC_SEQ = 32
PAST_LEN = 1024

CHUNK = 64
N_PAST_CHUNKS = 8
BAND = (N_PAST_CHUNKS + 1) * CHUNK
ATT_WINDOW = N_PAST_CHUNKS * CHUNK
D_CONV = D_MODEL // 2
D_ATTN = D_MODEL // 2
N_HEADS = 8
HEAD_DIM = D_ATTN // N_HEADS
CONV_WIDTH = 31
MAX_REL = 128
N_IN = 3 * D_CONV + 4 * D_ATTN + 2 * D_MODEL
EPS = 1e-6

kernel_name = "chunk_stream_conformer_hybrid_step"


def rms_norm(x, g):
    xf = x.astype(jnp.float32)
    y = xf * lax.rsqrt(jnp.mean(xf * xf, axis=-1, keepdims=True) + EPS)
    return (y * g.astype(jnp.float32)).astype(x.dtype)


def layer_norm(x, g, b):
    xf = x.astype(jnp.float32)
    mu = jnp.mean(xf, axis=-1, keepdims=True)
    var = jnp.mean(jnp.square(xf - mu), axis=-1, keepdims=True)
    y = (xf - mu) * lax.rsqrt(var + EPS)
    return (y * g.astype(jnp.float32) + b.astype(jnp.float32)).astype(x.dtype)


def layer_inputs(x, c, g_pre, w_mod, b_mod, w_in):
    bsz, L, _ = x.shape
    mod = c @ w_mod + b_mod
    shift, scale, gate = jnp.split(mod, 3, axis=-1)
    h = rms_norm(x, g_pre) * (1 + scale[:, None, :]) + shift[:, None, :]
    z = h @ w_in
    cuts = np.cumsum([D_CONV, D_CONV, D_CONV, D_ATTN, D_ATTN, D_ATTN, D_ATTN, D_MODEL])
    a, b, z_conv, q, k, v, z_attn, g_conv, g_attn = jnp.split(z, cuts, axis=-1)
    u = a * jax.nn.sigmoid(b)
    heads = lambda t: t.reshape(bsz, L, N_HEADS, HEAD_DIM)
    return u, z_conv, heads(q), heads(k), heads(v), z_attn, g_conv, g_attn, gate


def conv_branch(u_hist, z_conv, dw_w, dw_b, ln_g, ln_b, w_conv_out):
    y = lax.conv_general_dilated(u_hist, dw_w[:, None, :], window_strides=(1,), padding='VALID',
                                 dimension_numbers=('NWC', 'WIO', 'NWC'),
                                 feature_group_count=D_CONV) + dw_b
    y = jax.nn.silu(layer_norm(y, ln_g, ln_b)) * jax.nn.silu(z_conv)
    return y @ w_conv_out


def band_bias_mask(q_pos, k_pos, table):
    rel = q_pos[..., :, None] - k_pos[..., None, :]
    idx = jnp.clip(rel, -MAX_REL, MAX_REL) + MAX_REL
    bias = jnp.moveaxis(table.astype(jnp.float32)[:, idx], 0, -3)
    qc = (q_pos // CHUNK)[..., :, None]
    kc = (k_pos // CHUNK)[..., None, :]
    mask = (k_pos[..., None, :] >= 0) & (kc <= qc) & (kc >= qc - N_PAST_CHUNKS)
    return bias, mask[..., None, :, :]


def attend(q, k, v, bias, mask):
    s = jnp.einsum('...qhd,...khd->...hqk', q, k).astype(jnp.float32) * (HEAD_DIM ** -0.5) + bias
    s = jnp.where(mask, s, -1e30)
    p = jax.nn.softmax(s, axis=-1).astype(v.dtype)
    return jnp.einsum('...hqk,...khd->...qhd', p, v)


def prompt_band_attention(q, k, v, table):
    _, S, _, _ = q.shape
    n_chunks = S // CHUNK
    q_pos = jnp.arange(S).reshape(n_chunks, CHUNK)
    k_pos = (jnp.arange(n_chunks)[:, None] - N_PAST_CHUNKS) * CHUNK + jnp.arange(BAND)[None, :]
    bias, mask = band_bias_mask(q_pos, k_pos, table)
    band_idx = jnp.arange(n_chunks)[:, None] + jnp.arange(N_PAST_CHUNKS + 1)[None, :]

    def gather_band(t):
        pad = jnp.zeros((N_PAST_CHUNKS * CHUNK,) + t.shape[1:], t.dtype)
        tc = jnp.concatenate([pad, t], axis=0).reshape(n_chunks + N_PAST_CHUNKS, CHUNK, N_HEADS, HEAD_DIM)
        return tc[band_idx].reshape(n_chunks, BAND, N_HEADS, HEAD_DIM)

    def one_stream(args):
        qs, ks, vs = args
        o = attend(qs.reshape(n_chunks, CHUNK, N_HEADS, HEAD_DIM), gather_band(ks), gather_band(vs), bias, mask)
        return o.reshape(S, N_HEADS, HEAD_DIM)

    return lax.map(one_stream, (q, k, v))


def sample_band_attention(q, k_all, v_all, table):
    T = q.shape[1]
    W = k_all.shape[1] - T
    q_pos = PAST_LEN + jnp.arange(T)
    k_pos = PAST_LEN - W + jnp.arange(W + T)
    bias, mask = band_bias_mask(q_pos, k_pos, table)
    return attend(q, k_all, v_all, bias, mask)


def attn_branch_out(o, z_attn, w_attn_out):
    bsz, L = o.shape[:2]
    return (o.reshape(bsz, L, D_ATTN) * jax.nn.silu(z_attn)) @ w_attn_out


def layer_output(x, y_conv, y_attn, g_conv, g_attn, gate, w_o, g_post):
    merged = jax.nn.sigmoid(g_conv) * y_conv + jax.nn.sigmoid(g_attn) * y_attn
    out = rms_norm(merged @ w_o, g_post)
    return x + gate[:, None, :] * out


def setup_inputs(seed: int = 0) -> dict:
    key = jax.random.key(seed)
    ks = jax.random.split(key, 20)
    nrm = lambda k, shape, s: jax.random.normal(k, shape, jnp.float32) * s
    W = min(ATT_WINDOW, PAST_LEN)
    return {
        "x_prompt": nrm(ks[0], (BATCH, SEQ, D_MODEL), 1.0),
        "x_sample": nrm(ks[1], (DEC_BATCH, DEC_SEQ, D_MODEL), 1.0),
        "c_prompt": nrm(ks[2], (BATCH, D_MODEL), 1.0),
        "c_sample": nrm(ks[3], (DEC_BATCH, D_MODEL), 1.0),
        "cache_conv": nrm(ks[4], (DEPTH, DEC_BATCH, CONV_WIDTH - 1, D_CONV), 0.5),
        "cache_k": nrm(ks[5], (DEPTH, DEC_BATCH, W, N_HEADS, HEAD_DIM), 1.0),
        "cache_v": nrm(ks[6], (DEPTH, DEC_BATCH, W, N_HEADS, HEAD_DIM), 1.0),
        "g_pre": 1.0 + nrm(ks[7], (DEPTH, D_MODEL), 0.05),
        "g_post": 1.0 + nrm(ks[8], (DEPTH, D_MODEL), 0.05),
        "w_mod": nrm(ks[9], (DEPTH, D_MODEL, 3 * D_MODEL), 0.2 * D_MODEL ** -0.5),
        "b_mod": nrm(ks[10], (DEPTH, 3 * D_MODEL), 0.02),
        "w_in": nrm(ks[11], (DEPTH, D_MODEL, N_IN), D_MODEL ** -0.5),
        "dw_w": nrm(ks[12], (DEPTH, CONV_WIDTH, D_CONV), CONV_WIDTH ** -0.5),
        "dw_b": nrm(ks[13], (DEPTH, D_CONV), 0.02),
        "ln_g": 1.0 + nrm(ks[14], (DEPTH, D_CONV), 0.05),
        "ln_b": nrm(ks[15], (DEPTH, D_CONV), 0.02),
        "w_conv_out": nrm(ks[16], (DEPTH, D_CONV, D_MODEL), D_CONV ** -0.5),
        "rel_bias": nrm(ks[17], (DEPTH, N_HEADS, 2 * MAX_REL + 1), 0.5),
        "w_attn_out": nrm(ks[18], (DEPTH, D_ATTN, D_MODEL), D_ATTN ** -0.5),
        "w_o": nrm(ks[19], (DEPTH, D_MODEL, D_MODEL), D_MODEL ** -0.5),
    }


def reference(x_prompt, x_sample, c_prompt, c_sample, cache_conv, cache_k, cache_v,
              g_pre, g_post, w_mod, b_mod, w_in, dw_w, dw_b, ln_g, ln_b,
              w_conv_out, rel_bias, w_attn_out, w_o):
    xp, xs = x_prompt, x_sample
    conv_p, k_p, v_p, conv_s, k_s, v_s = [], [], [], [], [], []
    for l in range(DEPTH):
        u, zc, q, k, v, za, gc, ga, gate = layer_inputs(xp, c_prompt, g_pre[l], w_mod[l], b_mod[l], w_in[l])
        u_hist = jnp.concatenate([jnp.zeros((u.shape[0], CONV_WIDTH - 1, D_CONV), u.dtype), u], axis=1)
        y_conv = conv_branch(u_hist, zc, dw_w[l], dw_b[l], ln_g[l], ln_b[l], w_conv_out[l])
        o = prompt_band_attention(q, k, v, rel_bias[l])
        y_attn = attn_branch_out(o, za, w_attn_out[l])
        xp = layer_output(xp, y_conv, y_attn, gc, ga, gate, w_o[l], g_post[l])
        wp = min(ATT_WINDOW, k.shape[1])
        conv_p.append(u_hist[:, -(CONV_WIDTH - 1):])
        k_p.append(k[:, -wp:])
        v_p.append(v[:, -wp:])

        u, zc, q, k, v, za, gc, ga, gate = layer_inputs(xs, c_sample, g_pre[l], w_mod[l], b_mod[l], w_in[l])
        u_hist = jnp.concatenate([cache_conv[l], u], axis=1)
        y_conv = conv_branch(u_hist, zc, dw_w[l], dw_b[l], ln_g[l], ln_b[l], w_conv_out[l])
        k_all = jnp.concatenate([cache_k[l], k], axis=1)
        v_all = jnp.concatenate([cache_v[l], v], axis=1)
        o = sample_band_attention(q, k_all, v_all, rel_bias[l])
        y_attn = attn_branch_out(o, za, w_attn_out[l])
        xs = layer_output(xs, y_conv, y_attn, gc, ga, gate, w_o[l], g_post[l])
        ws = cache_k.shape[2]
        conv_s.append(u_hist[:, -(CONV_WIDTH - 1):])
        k_s.append(k_all[:, -ws:])
        v_s.append(v_all[:, -ws:])

    new_conv_prompt = jnp.stack(conv_p)
    new_k_prompt = jnp.stack(k_p)
    new_v_prompt = jnp.stack(v_p)
    new_conv_sample = jnp.stack(conv_s)
    new_k_sample = jnp.stack(k_s)
    new_v_sample = jnp.stack(v_s)
    return (xp, xs, new_conv_prompt, new_k_prompt, new_v_prompt, new_conv_sample, new_k_sample, new_v_sample)
```

```python
import functools

import jax
import jax.numpy as jnp
from jax import lax
from jax.experimental import pallas as pl
from jax.experimental.pallas import tpu as pltpu

D_MODEL = 1024
CHUNK = 64
N_PAST_CHUNKS = 8
ATT_WINDOW = N_PAST_CHUNKS * CHUNK
BAND = ATT_WINDOW + CHUNK
D_CONV = 512
D_ATTN = 512
N_HEADS = 8
HEAD_DIM = 64
CONV_WIDTH = 31
CONV_HIST = CONV_WIDTH - 1
CONV_PAD = 32
MAX_REL = 128
N_REL = 2 * MAX_REL + 1
EPS = 1e-6
NEG = -1e30

_C_A, _C_B, _C_ZC, _C_Q, _C_K, _C_V, _C_ZA, _C_GC, _C_GA = (
    0, 512, 1024, 1536, 2048, 2560, 3072, 3584, 4608)
N_IN = 5632

SEQ_TILE = 512
ROW_BLOCK = 128
VMEM_LIMIT_BYTES = 56 * 1024 * 1024

_BF16 = jnp.bfloat16
_F32 = jnp.float32


def _sigmoid(x):
    return jax.nn.sigmoid(x)


def _silu(x):
    return x * _sigmoid(x)


def _dot(a, b):
    return jnp.dot(a, b, preferred_element_type=_F32)


def _prep_kernel(table_ref, c_ref, wmod_ref, bmod_ref, mod_ref, bias_ref):
    mod_ref[...] = _dot(c_ref[...], wmod_ref[...]) + bmod_ref[...]

    qi = lax.broadcasted_iota(jnp.int32, (CHUNK, BAND), 0)
    kj = lax.broadcasted_iota(jnp.int32, (CHUNK, BAND), 1)
    idx = jnp.clip(ATT_WINDOW + qi - kj, -MAX_REL, MAX_REL) + MAX_REL
    for h in range(N_HEADS):
        def body(r, acc):
            return jnp.where(idx == r, table_ref[h, r], acc)
        bias_ref[h] = lax.fori_loop(0, N_REL, body, jnp.zeros((CHUNK, BAND), _F32))


def _prep(table, c_all, w_mod, b_mod):
    n = c_all.shape[0]
    return pl.pallas_call(
        _prep_kernel,
        out_shape=(jax.ShapeDtypeStruct((n, 3 * D_MODEL), _F32),
                   jax.ShapeDtypeStruct((N_HEADS, CHUNK, BAND), _F32)),
        in_specs=[pl.BlockSpec(memory_space=pltpu.SMEM),
                  pl.BlockSpec(memory_space=pltpu.VMEM),
                  pl.BlockSpec(memory_space=pltpu.VMEM),
                  pl.BlockSpec(memory_space=pltpu.VMEM)],
        out_specs=(pl.BlockSpec(memory_space=pltpu.VMEM),
                   pl.BlockSpec(memory_space=pltpu.VMEM)),
        compiler_params=pltpu.CompilerParams(vmem_limit_bytes=VMEM_LIMIT_BYTES),
        name="prep",
    )(table, c_all, w_mod, b_mod)


def _rms(x):
    return x * lax.rsqrt(jnp.mean(x * x, axis=-1, keepdims=True) + EPS)


def _conv_rows(ubuf_ref, rows, dww_ref):
    off = CONV_PAD - CONV_HIST
    cols = []
    for lb in range(D_CONV // 128):
        lanes = slice(lb * 128, (lb + 1) * 128)
        acc = jnp.zeros((rows, 128), _F32)
        for j in range(CONV_WIDTH):
            acc = acc + ubuf_ref[off + j:off + j + rows, lanes] * dww_ref[j:j + 1, lanes]
        cols.append(acc)
    return jnp.concatenate(cols, axis=-1)


def _attend(q_even, q_odd, kbuf_ref, vbuf_ref, kv_row0, n_q, q_row0, n_keys, bias_ref, maskrow):
    outs = []
    lane = lax.broadcasted_iota(jnp.int32, (n_q, 128), 1)
    for p in range(N_HEADS // 2):
        lanes = slice(p * 128, (p + 1) * 128)
        k2 = kbuf_ref[pl.ds(kv_row0, n_keys), lanes]
        v2 = vbuf_ref[pl.ds(kv_row0, n_keys), lanes]
        halves = []
        for half, qsrc in enumerate((q_even, q_odd)):
            qm = qsrc[q_row0:q_row0 + n_q, lanes]
            s = lax.dot_general(qm, k2, (((1,), (1,)), ((), ())), preferred_element_type=_F32)
            s = s + bias_ref[2 * p + half, 0:n_q, 0:n_keys]
            if maskrow is not None:
                s = s + maskrow
            m = jnp.max(s, axis=-1, keepdims=True)
            e = jnp.exp(s - m)
            l = jnp.sum(e, axis=-1, keepdims=True)
            o2 = _dot(e.astype(_BF16), v2)
            halves.append(o2 / l)
        outs.append(jnp.where(lane < HEAD_DIM, halves[0], halves[1]))
    return jnp.concatenate(outs, axis=-1)


def _layer_rows(x, shift, scale1, gate, g_post, w, ubuf_ref, kbuf_ref, vbuf_ref,
                kv_new_row0, chunks, bias_ref, mask_base):
    rows = x.shape[0]
    (win_ref, dww_ref, dwb_ref, lng_ref, lnb_ref, wco_ref, wao_ref, wo_ref) = w
    h = (_rms(x) * scale1 + shift).astype(_BF16)

    a = _dot(h, win_ref[:, _C_A:_C_A + D_CONV])
    b = _dot(h, win_ref[:, _C_B:_C_B + D_CONV])
    u = a * _sigmoid(b)
    ubuf_ref[CONV_PAD:CONV_PAD + rows, :] = u
    y = _conv_rows(ubuf_ref, rows, dww_ref) + dwb_ref[...]
    ubuf_ref[0:CONV_PAD, :] = u[rows - CONV_PAD:, :]
    mu = jnp.mean(y, axis=-1, keepdims=True)
    yc = y - mu
    var = jnp.mean(yc * yc, axis=-1, keepdims=True)
    y = yc * lax.rsqrt(var + EPS) * lng_ref[...] + lnb_ref[...]
    zc = _dot(h, win_ref[:, _C_ZC:_C_ZC + D_CONV])
    y_conv = _dot((_silu(y) * _silu(zc)).astype(_BF16), wco_ref[...])
    g_conv = _sigmoid(_dot(h, win_ref[:, _C_GC:_C_GC + D_MODEL]))
    merged = g_conv * y_conv

    q = _dot(h, win_ref[:, _C_Q:_C_Q + D_ATTN]) * (HEAD_DIM ** -0.5)
    k = _dot(h, win_ref[:, _C_K:_C_K + D_ATTN])
    v = _dot(h, win_ref[:, _C_V:_C_V + D_ATTN])
    kbuf_ref[pl.ds(kv_new_row0, rows), :] = k.astype(_BF16)
    vbuf_ref[pl.ds(kv_new_row0, rows), :] = v.astype(_BF16)
    lane = lax.broadcasted_iota(jnp.int32, (rows, D_ATTN), 1)
    even = (lane & HEAD_DIM) == 0
    q_even = jnp.where(even, q, 0.0).astype(_BF16)
    q_odd = jnp.where(even, 0.0, q).astype(_BF16)
    o_parts = []
    for (q_row0, n_q, band_row0, n_keys) in chunks:
        maskrow = None
        if mask_base is not None:
            col = lax.broadcasted_iota(jnp.int32, (1, n_keys), 1)
            maskrow = jnp.where(mask_base + band_row0 + col >= 0, 0.0, NEG).astype(_F32)
        o_parts.append(_attend(q_even, q_odd, kbuf_ref, vbuf_ref, band_row0, n_q, q_row0,
                               n_keys, bias_ref, maskrow))
    o = o_parts[0] if len(o_parts) == 1 else jnp.concatenate(o_parts, axis=0)
    za = _dot(h, win_ref[:, _C_ZA:_C_ZA + D_ATTN])
    y_attn = _dot((o * _silu(za)).astype(_BF16), wao_ref[...])
    g_attn = _sigmoid(_dot(h, win_ref[:, _C_GA:_C_GA + D_MODEL]))
    merged = merged + g_attn * y_attn

    out = _rms(_dot(merged.astype(_BF16), wo_ref[...])) * g_post
    return x + gate * out, k, v, u


def _prompt_kernel(x_ref, mod_ref, bias_ref, gpre_ref, gpost_ref, win_ref, dww_ref, dwb_ref,
                   lng_ref, lnb_ref, wco_ref, wao_ref, wo_ref,
                   y_ref, uo_ref, ko_ref, vo_ref,
                   ubuf_ref, kbuf_ref, vbuf_ref):
    t = pl.program_id(1)

    @pl.when(t == 0)
    def _():
        ubuf_ref[0:CONV_PAD, :] = jnp.zeros((CONV_PAD, D_CONV), _F32)
        kbuf_ref[0:ATT_WINDOW, :] = jnp.zeros((ATT_WINDOW, D_ATTN), _BF16)
        vbuf_ref[0:ATT_WINDOW, :] = jnp.zeros((ATT_WINDOW, D_ATTN), _BF16)

    shift = mod_ref[0, 0:1, :]
    scale1 = (1.0 + mod_ref[0, 1:2, :]) * gpre_ref[...]
    gate = mod_ref[0, 2:3, :]
    g_post = gpost_ref[...]
    w = (win_ref, dww_ref, dwb_ref, lng_ref, lnb_ref, wco_ref, wao_ref, wo_ref)
    mask_base = t * SEQ_TILE - ATT_WINDOW

    def body(rb, carry):
        r0 = pl.multiple_of(rb * ROW_BLOCK, ROW_BLOCK)
        x = x_ref[0, pl.ds(r0, ROW_BLOCK), :]
        chunks = [(c * CHUNK, CHUNK, r0 + c * CHUNK, BAND) for c in range(ROW_BLOCK // CHUNK)]
        y, k, v, u = _layer_rows(x, shift, scale1, gate, g_post, w, ubuf_ref,
                                 kbuf_ref, vbuf_ref, ATT_WINDOW + r0, chunks, bias_ref, mask_base)
        y_ref[0, pl.ds(r0, ROW_BLOCK), :] = y
        ko_ref[0, pl.ds(r0, ROW_BLOCK), :] = k
        vo_ref[0, pl.ds(r0, ROW_BLOCK), :] = v
        return carry

    lax.fori_loop(0, SEQ_TILE // ROW_BLOCK, body, 0)

    uo_ref[0] = ubuf_ref[0:CONV_PAD, :]
    kbuf_ref[0:ATT_WINDOW, :] = kbuf_ref[SEQ_TILE:SEQ_TILE + ATT_WINDOW, :]
    vbuf_ref[0:ATT_WINDOW, :] = vbuf_ref[SEQ_TILE:SEQ_TILE + ATT_WINDOW, :]


def _const_spec(shape):
    nd = len(shape)
    return pl.BlockSpec(shape, lambda *_: (0,) * nd, pipeline_mode=pl.Buffered(1))


def _prompt_call(x, mod3, bias, g_pre, g_post, w_in, dw_w, dw_b, ln_g, ln_b, w_co, w_ao, w_o):
    bsz, seq, _ = x.shape
    assert seq % SEQ_TILE == 0 and SEQ_TILE == ATT_WINDOW and SEQ_TILE % ROW_BLOCK == 0
    n_t = seq // SEQ_TILE
    row = lambda n: _const_spec((1, n))
    return pl.pallas_call(
        _prompt_kernel,
        grid=(bsz, n_t),
        out_shape=(jax.ShapeDtypeStruct((bsz, seq, D_MODEL), _F32),
                   jax.ShapeDtypeStruct((bsz, CONV_PAD, D_CONV), _F32),
                   jax.ShapeDtypeStruct((bsz, ATT_WINDOW, D_ATTN), _F32),
                   jax.ShapeDtypeStruct((bsz, ATT_WINDOW, D_ATTN), _F32)),
        in_specs=[pl.BlockSpec((1, SEQ_TILE, D_MODEL), lambda b, t: (b, t, 0)),
                  pl.BlockSpec((1, 3, D_MODEL), lambda b, t: (b, 0, 0)),
                  _const_spec((N_HEADS, CHUNK, BAND)),
                  row(D_MODEL), row(D_MODEL),
                  _const_spec((D_MODEL, N_IN)),
                  _const_spec((CONV_WIDTH, D_CONV)),
                  row(D_CONV), row(D_CONV), row(D_CONV),
                  _const_spec((D_CONV, D_MODEL)),
                  _const_spec((D_ATTN, D_MODEL)),
                  _const_spec((D_MODEL, D_MODEL))],
        out_specs=(pl.BlockSpec((1, SEQ_TILE, D_MODEL), lambda b, t: (b, t, 0)),
                   pl.BlockSpec((1, CONV_PAD, D_CONV), lambda b, t: (b, 0, 0)),
                   pl.BlockSpec((1, ATT_WINDOW, D_ATTN), lambda b, t: (b, 0, 0)),
                   pl.BlockSpec((1, ATT_WINDOW, D_ATTN), lambda b, t: (b, 0, 0))),
        scratch_shapes=[pltpu.VMEM((CONV_PAD + ROW_BLOCK, D_CONV), _F32),
                        pltpu.VMEM((ATT_WINDOW + SEQ_TILE, D_ATTN), _BF16),
                        pltpu.VMEM((ATT_WINDOW + SEQ_TILE, D_ATTN), _BF16)],
        compiler_params=pltpu.CompilerParams(
            dimension_semantics=("arbitrary", "arbitrary"),
            vmem_limit_bytes=VMEM_LIMIT_BYTES),
        name="prompt_layer",
    )(x, mod3, bias, g_pre, g_post, w_in, dw_w, dw_b, ln_g, ln_b, w_co, w_ao, w_o)


def _sample_kernel(n_new, x_ref, mod_ref, bias_ref, cconv_ref, ck_ref, cv_ref, gpre_ref, gpost_ref,
                   win_ref, dww_ref, dwb_ref, lng_ref, lnb_ref, wco_ref, wao_ref, wo_ref,
                   y_ref, uo_ref, ko_ref, vo_ref,
                   ubuf_ref, kbuf_ref, vbuf_ref):
    ubuf_ref[0:CONV_PAD, :] = cconv_ref[0]
    ck = ck_ref[0]
    cv = cv_ref[0]
    kbuf_ref[0:ATT_WINDOW, :] = ck.astype(_BF16)
    vbuf_ref[0:ATT_WINDOW, :] = cv.astype(_BF16)

    shift = mod_ref[0, 0:1, :]
    scale1 = (1.0 + mod_ref[0, 1:2, :]) * gpre_ref[...]
    gate = mod_ref[0, 2:3, :]
    w = (win_ref, dww_ref, dwb_ref, lng_ref, lnb_ref, wco_ref, wao_ref, wo_ref)
    chunks = [(0, n_new, 0, ATT_WINDOW + n_new)]
    y, k, v, u = _layer_rows(x_ref[0], shift, scale1, gate, gpost_ref[...], w, ubuf_ref,
                             kbuf_ref, vbuf_ref, ATT_WINDOW, chunks, bias_ref, None)
    y_ref[0] = y
    uo_ref[0] = u
    ko_ref[0, 0:ATT_WINDOW - n_new, :] = ck[n_new:, :]
    ko_ref[0, ATT_WINDOW - n_new:, :] = k
    vo_ref[0, 0:ATT_WINDOW - n_new, :] = cv[n_new:, :]
    vo_ref[0, ATT_WINDOW - n_new:, :] = v


def _sample_call(x, mod3, bias, cconv, ck, cv, g_pre, g_post, w_in, dw_w, dw_b, ln_g, ln_b,
                 w_co, w_ao, w_o):
    bsz, n_new, _ = x.shape
    assert ck.shape[1] == ATT_WINDOW and n_new % 16 == 0 and CONV_HIST <= n_new <= CHUNK
    row = lambda n: _const_spec((1, n))
    per_b = lambda *s: pl.BlockSpec((1,) + s, lambda b: (b,) + (0,) * len(s))
    return pl.pallas_call(
        functools.partial(_sample_kernel, n_new),
        grid=(bsz,),
        out_shape=(jax.ShapeDtypeStruct((bsz, n_new, D_MODEL), _F32),
                   jax.ShapeDtypeStruct((bsz, n_new, D_CONV), _F32),
                   jax.ShapeDtypeStruct((bsz, ATT_WINDOW, D_ATTN), _F32),
                   jax.ShapeDtypeStruct((bsz, ATT_WINDOW, D_ATTN), _F32)),
        in_specs=[per_b(n_new, D_MODEL), per_b(3, D_MODEL),
                  _const_spec((N_HEADS, CHUNK, BAND)),
                  per_b(CONV_PAD, D_CONV), per_b(ATT_WINDOW, D_ATTN), per_b(ATT_WINDOW, D_ATTN),
                  row(D_MODEL), row(D_MODEL),
                  _const_spec((D_MODEL, N_IN)),
                  _const_spec((CONV_WIDTH, D_CONV)),
                  row(D_CONV), row(D_CONV), row(D_CONV),
                  _const_spec((D_CONV, D_MODEL)),
                  _const_spec((D_ATTN, D_MODEL)),
                  _const_spec((D_MODEL, D_MODEL))],
        out_specs=(per_b(n_new, D_MODEL), per_b(n_new, D_CONV),
                   per_b(ATT_WINDOW, D_ATTN), per_b(ATT_WINDOW, D_ATTN)),
        scratch_shapes=[pltpu.VMEM((CONV_PAD + n_new, D_CONV), _F32),
                        pltpu.VMEM((ATT_WINDOW + n_new, D_ATTN), _BF16),
                        pltpu.VMEM((ATT_WINDOW + n_new, D_ATTN), _BF16)],
        compiler_params=pltpu.CompilerParams(
            dimension_semantics=("arbitrary",),
            vmem_limit_bytes=VMEM_LIMIT_BYTES),
        name="sample_layer",
    )(x, mod3, bias, cconv, ck, cv, g_pre, g_post, w_in, dw_w, dw_b, ln_g, ln_b, w_co, w_ao, w_o)


def kernel(x_prompt, x_sample, c_prompt, c_sample, cache_conv, cache_k, cache_v, g_pre, g_post,
           w_mod, b_mod, w_in, dw_w, dw_b, ln_g, ln_b, w_conv_out, rel_bias, w_attn_out, w_o):
    depth = g_pre.shape[0]
    assert depth == 1
    l = 0
    bsz, seq, _ = x_prompt.shape
    dbsz, dseq, _ = x_sample.shape

    c_all = jnp.concatenate([c_prompt, c_sample], axis=0)
    mod, bias = _prep(rel_bias[l], c_all, w_mod[l], b_mod[l][None, :])
    mod3 = mod.reshape(bsz + dbsz, 3, D_MODEL)

    weights = (g_pre[l][None, :], g_post[l][None, :], w_in[l].astype(_BF16), dw_w[l],
               dw_b[l][None, :], ln_g[l][None, :], ln_b[l][None, :],
               w_conv_out[l].astype(_BF16), w_attn_out[l].astype(_BF16), w_o[l].astype(_BF16))

    y_p, u_p, k_p, v_p = _prompt_call(x_prompt, mod3[:bsz], bias, *weights)

    w_hist = cache_k.shape[2]
    cconv = jnp.pad(cache_conv[l], ((0, 0), (CONV_PAD - CONV_HIST, 0), (0, 0)))
    y_s, u_s, k_s, v_s = _sample_call(
        x_sample, mod3[bsz:], bias, cconv,
        cache_k[l].reshape(dbsz, w_hist, D_ATTN), cache_v[l].reshape(dbsz, w_hist, D_ATTN),
        *weights)

    heads = lambda t: t.reshape(1, t.shape[0], t.shape[1], N_HEADS, HEAD_DIM)
    return (y_p, y_s,
            u_p[None, :, CONV_PAD - CONV_HIST:, :], heads(k_p), heads(v_p),
            u_s[None, :, dseq - CONV_HIST:, :], heads(k_s), heads(v_s))
```

```python
import functools

import jax
import jax.numpy as jnp
from jax import lax
from jax.experimental import pallas as pl
from jax.experimental.pallas import tpu as pltpu

D_MODEL = 1024
CHUNK = 64
N_PAST_CHUNKS = 8
ATT_WINDOW = N_PAST_CHUNKS * CHUNK
BAND = ATT_WINDOW + CHUNK
D_CONV = 512
D_ATTN = 512
N_HEADS = 8
HEAD_DIM = 64
CONV_WIDTH = 31
CONV_HIST = CONV_WIDTH - 1
CONV_PAD = 32
SUBLANES = 8
LANES = 128
MAX_REL = 128
N_REL = 2 * MAX_REL + 1
EPS = 1e-6
NEG = -1e30

_C_A, _C_B, _C_ZC, _C_Q, _C_K, _C_V, _C_ZA, _C_GC, _C_GA = (
    0, 512, 1024, 1536, 2048, 2560, 3072, 3584, 4608)
N_IN = 5632

SEQ_TILE = 512
ROW_BLOCK = 256
CONV_SUB_ROWS = 32
VMEM_LIMIT_BYTES = 56 * 1024 * 1024

_BF16 = jnp.bfloat16
_F32 = jnp.float32


def _sigmoid(x):
    return jax.nn.sigmoid(x)


def _silu(x):
    return x * _sigmoid(x)


def _dot(a, b):
    return jnp.dot(a, b, preferred_element_type=_F32)


def _prep_kernel(table_ref, c_ref, wmod_ref, bmod_ref, mod_ref, bias_ref):
    mod_ref[...] = _dot(c_ref[...], wmod_ref[...]) + bmod_ref[...]

    qi = lax.broadcasted_iota(jnp.int32, (CHUNK, BAND), 0)
    kj = lax.broadcasted_iota(jnp.int32, (CHUNK, BAND), 1)
    idx = jnp.clip(ATT_WINDOW + qi - kj, -MAX_REL, MAX_REL) + MAX_REL
    for h in range(N_HEADS):
        def body(r, acc):
            return jnp.where(idx == r, table_ref[h, r], acc)
        bias_ref[h] = lax.fori_loop(0, N_REL, body, jnp.zeros((CHUNK, BAND), _F32))


def _prep(table, c_all, w_mod, b_mod):
    n = c_all.shape[0]
    return pl.pallas_call(
        _prep_kernel,
        out_shape=(jax.ShapeDtypeStruct((n, 3 * D_MODEL), _F32),
                   jax.ShapeDtypeStruct((N_HEADS, CHUNK, BAND), _F32)),
        in_specs=[pl.BlockSpec(memory_space=pltpu.SMEM),
                  pl.BlockSpec(memory_space=pltpu.VMEM),
                  pl.BlockSpec(memory_space=pltpu.VMEM),
                  pl.BlockSpec(memory_space=pltpu.VMEM)],
        out_specs=(pl.BlockSpec(memory_space=pltpu.VMEM),
                   pl.BlockSpec(memory_space=pltpu.VMEM)),
        compiler_params=pltpu.CompilerParams(vmem_limit_bytes=VMEM_LIMIT_BYTES),
        name="prep",
    )(table, c_all, w_mod, b_mod)


def _rms(x):
    return x * lax.rsqrt(jnp.mean(x * x, axis=-1, keepdims=True) + EPS)


def _conv_rows(ubuf_ref, ush_ref, rows, dww_ref):
    span = rows + CONV_PAD - SUBLANES
    for r in range(1, SUBLANES):
        ush_ref[r - 1, 0:span, :] = ubuf_ref[r:r + span, :]
    first = CONV_PAD - CONV_HIST
    sub = min(rows, CONV_SUB_ROWS)
    row_parts = []
    for rs in range(0, rows, sub):
        cols = []
        for lb in range(D_CONV // LANES):
            lanes = slice(lb * LANES, (lb + 1) * LANES)
            acc = None
            for j in range(CONV_WIDTH):
                r = (first + j) % SUBLANES
                base = rs + first + j - r
                if r == 0:
                    tap = ubuf_ref[base:base + sub, lanes]
                else:
                    tap = ush_ref[r - 1, base:base + sub, lanes]
                term = tap * dww_ref[j:j + 1, lanes]
                acc = term if acc is None else acc + term
            cols.append(acc)
        row_parts.append(jnp.concatenate(cols, axis=-1))
    return row_parts[0] if len(row_parts) == 1 else jnp.concatenate(row_parts, axis=0)


def _attn_scores(q_even, q_odd, kbuf_ref, chunks, bias_ref, mask_base):
    scores = []
    for (q_row0, n_q, band_row0, n_keys) in chunks:
        maskrow = None
        if mask_base is not None:
            col = lax.broadcasted_iota(jnp.int32, (1, n_keys), 1)
            maskrow = jnp.where(mask_base + band_row0 + col >= 0, 0.0, NEG).astype(_F32)
        for p in range(N_HEADS // 2):
            lanes = slice(p * LANES, (p + 1) * LANES)
            k2 = kbuf_ref[pl.ds(band_row0, n_keys), lanes]
            for half, qsrc in enumerate((q_even, q_odd)):
                qm = qsrc[q_row0:q_row0 + n_q, lanes]
                s = lax.dot_general(qm, k2, (((1,), (1,)), ((), ())), preferred_element_type=_F32)
                s = s + bias_ref[2 * p + half, 0:n_q, 0:n_keys]
                if maskrow is not None:
                    s = s + maskrow
                scores.append(s)
    return scores


def _attn_probs(scores):
    probs = []
    for s in scores:
        m = jnp.max(s, axis=-1, keepdims=True)
        e = jnp.exp(s - m)
        l = jnp.sum(e, axis=-1, keepdims=True)
        probs.append((e.astype(_BF16), l))
    return probs


def _attn_values(probs, vbuf_ref, chunks):
    o_parts = []
    it = iter(probs)
    for (q_row0, n_q, band_row0, n_keys) in chunks:
        lane = lax.broadcasted_iota(jnp.int32, (n_q, LANES), 1)
        outs = []
        for p in range(N_HEADS // 2):
            lanes = slice(p * LANES, (p + 1) * LANES)
            v2 = vbuf_ref[pl.ds(band_row0, n_keys), lanes]
            halves = []
            for half in range(2):
                e, l = next(it)
                halves.append(_dot(e, v2) / l)
            outs.append(jnp.where(lane < HEAD_DIM, halves[0], halves[1]))
        o_parts.append(jnp.concatenate(outs, axis=-1))
    return o_parts[0] if len(o_parts) == 1 else jnp.concatenate(o_parts, axis=0)


def _layer_rows(x, shift, scale1, gate, g_post, w, ubuf_ref, ush_ref, kbuf_ref, vbuf_ref,
                kv_new_row0, chunks, bias_ref, mask_base):
    rows = x.shape[0]
    (win_ref, dww_ref, dwb_ref, lng_ref, lnb_ref, wco_ref, wao_ref, wo_ref) = w
    h = (_rms(x) * scale1 + shift).astype(_BF16)

    ab = _dot(h, win_ref[:, _C_A:_C_A + 2 * D_CONV])
    u = ab[:, :D_CONV] * _sigmoid(ab[:, D_CONV:])
    ubuf_ref[CONV_PAD:CONV_PAD + rows, :] = u

    qkv = _dot(h, win_ref[:, _C_Q:_C_Q + 3 * D_ATTN])
    q = qkv[:, :D_ATTN] * (HEAD_DIM ** -0.5)
    k = qkv[:, D_ATTN:2 * D_ATTN]
    v = qkv[:, 2 * D_ATTN:]
    kbuf_ref[pl.ds(kv_new_row0, rows), :] = k.astype(_BF16)
    vbuf_ref[pl.ds(kv_new_row0, rows), :] = v.astype(_BF16)
    lane = lax.broadcasted_iota(jnp.int32, (rows, D_ATTN), 1)
    even = (lane & HEAD_DIM) == 0
    q_even = jnp.where(even, q, 0.0).astype(_BF16)
    q_odd = jnp.where(even, 0.0, q).astype(_BF16)

    scores = _attn_scores(q_even, q_odd, kbuf_ref, chunks, bias_ref, mask_base)

    zc = _dot(h, win_ref[:, _C_ZC:_C_ZC + D_CONV])
    zag = _dot(h, win_ref[:, _C_ZA:N_IN])
    za = zag[:, :D_ATTN]
    g_conv = _sigmoid(zag[:, _C_GC - _C_ZA:_C_GA - _C_ZA])
    g_attn = _sigmoid(zag[:, _C_GA - _C_ZA:])

    y = _conv_rows(ubuf_ref, ush_ref, rows, dww_ref) + dwb_ref[...]
    ubuf_ref[0:CONV_PAD, :] = u[rows - CONV_PAD:, :]
    mu = jnp.mean(y, axis=-1, keepdims=True)
    yc = y - mu
    var = jnp.mean(yc * yc, axis=-1, keepdims=True)
    y = yc * lax.rsqrt(var + EPS) * lng_ref[...] + lnb_ref[...]
    conv_act = (_silu(y) * _silu(zc)).astype(_BF16)

    probs = _attn_probs(scores)

    y_conv = _dot(conv_act, wco_ref[...])
    o = _attn_values(probs, vbuf_ref, chunks)
    y_attn = _dot((o * _silu(za)).astype(_BF16), wao_ref[...])
    merged = g_conv * y_conv + g_attn * y_attn

    out = _rms(_dot(merged.astype(_BF16), wo_ref[...])) * g_post
    return x + gate * out, k, v, u


def _prompt_kernel(x_ref, mod_ref, bias_ref, gpre_ref, gpost_ref, win_ref, dww_ref, dwb_ref,
                   lng_ref, lnb_ref, wco_ref, wao_ref, wo_ref,
                   y_ref, uo_ref, ko_ref, vo_ref,
                   ubuf_ref, ush_ref, kbuf_ref, vbuf_ref):
    t = pl.program_id(1)

    @pl.when(t == 0)
    def _():
        ubuf_ref[0:CONV_PAD, :] = jnp.zeros((CONV_PAD, D_CONV), _F32)
        kbuf_ref[0:ATT_WINDOW, :] = jnp.zeros((ATT_WINDOW, D_ATTN), _BF16)
        vbuf_ref[0:ATT_WINDOW, :] = jnp.zeros((ATT_WINDOW, D_ATTN), _BF16)

    shift = mod_ref[0, 0:1, :]
    scale1 = (1.0 + mod_ref[0, 1:2, :]) * gpre_ref[...]
    gate = mod_ref[0, 2:3, :]
    g_post = gpost_ref[...]
    w = (win_ref, dww_ref, dwb_ref, lng_ref, lnb_ref, wco_ref, wao_ref, wo_ref)
    mask_base = t * SEQ_TILE - ATT_WINDOW

    def body(rb, carry):
        r0 = pl.multiple_of(rb * ROW_BLOCK, ROW_BLOCK)
        x = x_ref[0, pl.ds(r0, ROW_BLOCK), :]
        chunks = [(c * CHUNK, CHUNK, r0 + c * CHUNK, BAND) for c in range(ROW_BLOCK // CHUNK)]
        y, k, v, u = _layer_rows(x, shift, scale1, gate, g_post, w, ubuf_ref, ush_ref,
                                 kbuf_ref, vbuf_ref, ATT_WINDOW + r0, chunks, bias_ref, mask_base)
        y_ref[0, pl.ds(r0, ROW_BLOCK), :] = y
        ko_ref[0, pl.ds(r0, ROW_BLOCK), :] = k
        vo_ref[0, pl.ds(r0, ROW_BLOCK), :] = v
        return carry

    lax.fori_loop(0, SEQ_TILE // ROW_BLOCK, body, 0)

    uo_ref[0] = ubuf_ref[0:CONV_PAD, :]
    kbuf_ref[0:ATT_WINDOW, :] = kbuf_ref[SEQ_TILE:SEQ_TILE + ATT_WINDOW, :]
    vbuf_ref[0:ATT_WINDOW, :] = vbuf_ref[SEQ_TILE:SEQ_TILE + ATT_WINDOW, :]


def _const_spec(shape):
    nd = len(shape)
    return pl.BlockSpec(shape, lambda *_: (0,) * nd, pipeline_mode=pl.Buffered(1))


def _prompt_call(x, mod3, bias, g_pre, g_post, w_in, dw_w, dw_b, ln_g, ln_b, w_co, w_ao, w_o):
    bsz, seq, _ = x.shape
    assert seq % SEQ_TILE == 0 and SEQ_TILE == ATT_WINDOW and SEQ_TILE % ROW_BLOCK == 0
    n_t = seq // SEQ_TILE
    row = lambda n: _const_spec((1, n))
    return pl.pallas_call(
        _prompt_kernel,
        grid=(bsz, n_t),
        out_shape=(jax.ShapeDtypeStruct((bsz, seq, D_MODEL), _F32),
                   jax.ShapeDtypeStruct((bsz, CONV_PAD, D_CONV), _F32),
                   jax.ShapeDtypeStruct((bsz, ATT_WINDOW, D_ATTN), _F32),
                   jax.ShapeDtypeStruct((bsz, ATT_WINDOW, D_ATTN), _F32)),
        in_specs=[pl.BlockSpec((1, SEQ_TILE, D_MODEL), lambda b, t: (b, t, 0)),
                  pl.BlockSpec((1, 3, D_MODEL), lambda b, t: (b, 0, 0)),
                  _const_spec((N_HEADS, CHUNK, BAND)),
                  row(D_MODEL), row(D_MODEL),
                  _const_spec((D_MODEL, N_IN)),
                  _const_spec((CONV_WIDTH, D_CONV)),
                  row(D_CONV), row(D_CONV), row(D_CONV),
                  _const_spec((D_CONV, D_MODEL)),
                  _const_spec((D_ATTN, D_MODEL)),
                  _const_spec((D_MODEL, D_MODEL))],
        out_specs=(pl.BlockSpec((1, SEQ_TILE, D_MODEL), lambda b, t: (b, t, 0)),
                   pl.BlockSpec((1, CONV_PAD, D_CONV), lambda b, t: (b, 0, 0)),
                   pl.BlockSpec((1, ATT_WINDOW, D_ATTN), lambda b, t: (b, 0, 0)),
                   pl.BlockSpec((1, ATT_WINDOW, D_ATTN), lambda b, t: (b, 0, 0))),
        scratch_shapes=[pltpu.VMEM((CONV_PAD + ROW_BLOCK, D_CONV), _F32),
                        pltpu.VMEM((SUBLANES - 1, CONV_PAD + ROW_BLOCK, D_CONV), _F32),
                        pltpu.VMEM((ATT_WINDOW + SEQ_TILE, D_ATTN), _BF16),
                        pltpu.VMEM((ATT_WINDOW + SEQ_TILE, D_ATTN), _BF16)],
        compiler_params=pltpu.CompilerParams(
            dimension_semantics=("arbitrary", "arbitrary"),
            vmem_limit_bytes=VMEM_LIMIT_BYTES),
        name="prompt_layer",
    )(x, mod3, bias, g_pre, g_post, w_in, dw_w, dw_b, ln_g, ln_b, w_co, w_ao, w_o)


def _sample_kernel(n_new, x_ref, mod_ref, bias_ref, cconv_ref, ck_ref, cv_ref, gpre_ref, gpost_ref,
                   win_ref, dww_ref, dwb_ref, lng_ref, lnb_ref, wco_ref, wao_ref, wo_ref,
                   y_ref, uo_ref, ko_ref, vo_ref,
                   ubuf_ref, ush_ref, kbuf_ref, vbuf_ref):
    ubuf_ref[0:CONV_PAD, :] = cconv_ref[0]
    ck = ck_ref[0]
    cv = cv_ref[0]
    kbuf_ref[0:ATT_WINDOW, :] = ck.astype(_BF16)
    vbuf_ref[0:ATT_WINDOW, :] = cv.astype(_BF16)

    shift = mod_ref[0, 0:1, :]
    scale1 = (1.0 + mod_ref[0, 1:2, :]) * gpre_ref[...]
    gate = mod_ref[0, 2:3, :]
    w = (win_ref, dww_ref, dwb_ref, lng_ref, lnb_ref, wco_ref, wao_ref, wo_ref)
    chunks = [(0, n_new, 0, ATT_WINDOW + n_new)]
    y, k, v, u = _layer_rows(x_ref[0], shift, scale1, gate, gpost_ref[...], w, ubuf_ref, ush_ref,
                             kbuf_ref, vbuf_ref, ATT_WINDOW, chunks, bias_ref, None)
    y_ref[0] = y
    uo_ref[0] = u
    ko_ref[0, 0:ATT_WINDOW - n_new, :] = ck[n_new:, :]
    ko_ref[0, ATT_WINDOW - n_new:, :] = k
    vo_ref[0, 0:ATT_WINDOW - n_new, :] = cv[n_new:, :]
    vo_ref[0, ATT_WINDOW - n_new:, :] = v


def _sample_call(x, mod3, bias, cconv, ck, cv, g_pre, g_post, w_in, dw_w, dw_b, ln_g, ln_b,
                 w_co, w_ao, w_o):
    bsz, n_new, _ = x.shape
    assert ck.shape[1] == ATT_WINDOW and n_new % 16 == 0 and CONV_PAD <= n_new <= CHUNK
    row = lambda n: _const_spec((1, n))
    per_b = lambda *s: pl.BlockSpec((1,) + s, lambda b: (b,) + (0,) * len(s))
    return pl.pallas_call(
        functools.partial(_sample_kernel, n_new),
        grid=(bsz,),
        out_shape=(jax.ShapeDtypeStruct((bsz, n_new, D_MODEL), _F32),
                   jax.ShapeDtypeStruct((bsz, n_new, D_CONV), _F32),
                   jax.ShapeDtypeStruct((bsz, ATT_WINDOW, D_ATTN), _F32),
                   jax.ShapeDtypeStruct((bsz, ATT_WINDOW, D_ATTN), _F32)),
        in_specs=[per_b(n_new, D_MODEL), per_b(3, D_MODEL),
                  _const_spec((N_HEADS, CHUNK, BAND)),
                  per_b(CONV_PAD, D_CONV), per_b(ATT_WINDOW, D_ATTN), per_b(ATT_WINDOW, D_ATTN),
                  row(D_MODEL), row(D_MODEL),
                  _const_spec((D_MODEL, N_IN)),
                  _const_spec((CONV_WIDTH, D_CONV)),
                  row(D_CONV), row(D_CONV), row(D_CONV),
                  _const_spec((D_CONV, D_MODEL)),
                  _const_spec((D_ATTN, D_MODEL)),
                  _const_spec((D_MODEL, D_MODEL))],
        out_specs=(per_b(n_new, D_MODEL), per_b(n_new, D_CONV),
                   per_b(ATT_WINDOW, D_ATTN), per_b(ATT_WINDOW, D_ATTN)),
        scratch_shapes=[pltpu.VMEM((CONV_PAD + n_new, D_CONV), _F32),
                        pltpu.VMEM((SUBLANES - 1, CONV_PAD + n_new, D_CONV), _F32),
                        pltpu.VMEM((ATT_WINDOW + n_new, D_ATTN), _BF16),
                        pltpu.VMEM((ATT_WINDOW + n_new, D_ATTN), _BF16)],
        compiler_params=pltpu.CompilerParams(
            dimension_semantics=("arbitrary",),
            vmem_limit_bytes=VMEM_LIMIT_BYTES),
        name="sample_layer",
    )(x, mod3, bias, cconv, ck, cv, g_pre, g_post, w_in, dw_w, dw_b, ln_g, ln_b, w_co, w_ao, w_o)


def kernel(x_prompt, x_sample, c_prompt, c_sample, cache_conv, cache_k, cache_v, g_pre, g_post,
           w_mod, b_mod, w_in, dw_w, dw_b, ln_g, ln_b, w_conv_out, rel_bias, w_attn_out, w_o):
    depth = g_pre.shape[0]
    assert depth == 1
    l = 0
    bsz, seq, _ = x_prompt.shape
    dbsz, dseq, _ = x_sample.shape

    c_all = jnp.concatenate([c_prompt, c_sample], axis=0)
    mod, bias = _prep(rel_bias[l], c_all, w_mod[l], b_mod[l][None, :])
    mod3 = mod.reshape(bsz + dbsz, 3, D_MODEL)

    weights = (g_pre[l][None, :], g_post[l][None, :], w_in[l].astype(_BF16), dw_w[l],
               dw_b[l][None, :], ln_g[l][None, :], ln_b[l][None, :],
               w_conv_out[l].astype(_BF16), w_attn_out[l].astype(_BF16), w_o[l].astype(_BF16))

    y_p, u_p, k_p, v_p = _prompt_call(x_prompt, mod3[:bsz], bias, *weights)

    w_hist = cache_k.shape[2]
    cconv = jnp.pad(cache_conv[l], ((0, 0), (CONV_PAD - CONV_HIST, 0), (0, 0)))
    y_s, u_s, k_s, v_s = _sample_call(
        x_sample, mod3[bsz:], bias, cconv,
        cache_k[l].reshape(dbsz, w_hist, D_ATTN), cache_v[l].reshape(dbsz, w_hist, D_ATTN),
        *weights)

    heads = lambda t: t.reshape(1, t.shape[0], t.shape[1], N_HEADS, HEAD_DIM)
    return (y_p, y_s,
            u_p[None, :, CONV_PAD - CONV_HIST:, :], heads(k_p), heads(v_p),
            u_s[None, :, dseq - CONV_HIST:, :], heads(k_s), heads(v_s))
```
